```python
import jax, jax.numpy as jnp
from jax import lax
import numpy as np

D_MODEL = 1024
BATCH = 2
SEQ = 16384
DEPTH = 2
DEC_BATCH = 4
DEC_SEQ = 4096
PAST_LEN = 128

N_HEADS = 8
N_KV_HEADS = 2
HEAD_DIM = D_MODEL // N_HEADS
GROUP = N_HEADS // N_KV_HEADS
QKV_DIM = (N_HEADS + 2 * N_KV_HEADS) * HEAD_DIM
ROPE_THETA = 10000.0
ROPE_SEG = HEAD_DIM // 2
ROPE_FREQS = ROPE_SEG // 2
GRID_W = 64
Q_BLOCK = 128
POOL_WINDOWS = (2, 4, 8, 16)
N_POOL_GROUPS = len(POOL_WINDOWS)
POOL_GROUP_DIM = D_MODEL // N_POOL_GROUPS
D_FF = 2816
CONV_W = 3
EPS = 1e-6
N_MIXERS = 2
N_ATTN_LAYERS = (DEPTH + 1) // 2
N_POOL_LAYERS = DEPTH // 2

kernel_name = 'hybrid_attn_pool_convffn_encoder'


def rmsnorm(x, g):
    xf = x.astype(jnp.float32)
    y = xf * lax.rsqrt(jnp.mean(xf * xf, axis=-1, keepdims=True) + EPS)
    return (y * g.astype(jnp.float32)).astype(x.dtype)


def axial_rope_tables(seq_len):
    rows = seq_len // GRID_W
    row = jnp.repeat(jnp.arange(rows, dtype=jnp.float32), GRID_W)
    col = jnp.tile(jnp.arange(GRID_W, dtype=jnp.float32), rows)
    inv = ROPE_THETA ** (-jnp.arange(ROPE_FREQS, dtype=jnp.float32) / ROPE_FREQS)
    ang = jnp.stack([row[:, None] * inv, col[:, None] * inv], axis=1)
    return jnp.cos(ang), jnp.sin(ang)


def apply_rope(x, cos, sin):
    b, s, h, d = x.shape
    xs = x.astype(jnp.float32).reshape(b, s, h, 2, 2, ROPE_FREQS)
    x1, x2 = xs[..., 0, :], xs[..., 1, :]
    c = cos[None, :, None]
    sn = sin[None, :, None]
    out = jnp.stack([x1 * c - x2 * sn, x2 * c + x1 * sn], axis=-2)
    return out.reshape(b, s, h, d).astype(x.dtype)


def attention_mixer(h, w_qkv, q_gain, k_gain, w_o):
    b, s, _ = h.shape
    qkv = h @ w_qkv
    q = qkv[..., :N_HEADS * HEAD_DIM].reshape(b, s, N_HEADS, HEAD_DIM)
    k = qkv[..., N_HEADS * HEAD_DIM:(N_HEADS + N_KV_HEADS) * HEAD_DIM].reshape(b, s, N_KV_HEADS, HEAD_DIM)
    v = qkv[..., (N_HEADS + N_KV_HEADS) * HEAD_DIM:].reshape(b, s, N_KV_HEADS, HEAD_DIM)
    q = rmsnorm(q, q_gain)
    k = rmsnorm(k, k_gain)
    cos, sin = axial_rope_tables(s)
    q = apply_rope(q, cos, sin)
    k = apply_rope(k, cos, sin)
    nb = s // Q_BLOCK
    qb = q.reshape(b, nb, Q_BLOCK, N_KV_HEADS, GROUP, HEAD_DIM).transpose(1, 0, 2, 3, 4, 5)
    scale = HEAD_DIM ** -0.5

    def block(qi):
        sc = jnp.einsum('bqkgd,bskd->bkgqs', qi, k, preferred_element_type=jnp.float32) * scale
        p = jax.nn.softmax(sc, axis=-1).astype(v.dtype)
        return jnp.einsum('bkgqs,bskd->bqkgd', p, v)

    ob = lax.map(block, qb)
    o = ob.transpose(1, 0, 2, 3, 4, 5).reshape(b, s, N_HEADS * HEAD_DIM)
    return o @ w_o


def pool_mixer(h, w_group, scale):
    b, s, d = h.shape
    hf = h.astype(jnp.float32).reshape(b, s, N_POOL_GROUPS, POOL_GROUP_DIM)
    csum = jnp.concatenate([jnp.zeros((b, 1, N_POOL_GROUPS, POOL_GROUP_DIM), jnp.float32),
                            jnp.cumsum(hf, axis=1)], axis=1)
    t = jnp.arange(s)
    means = []
    for gi, w in enumerate(POOL_WINDOWS):
        lo = jnp.clip(t - w // 2, 0, s)
        hi = jnp.clip(t + w // 2, 0, s)
        cnt = (hi - lo).astype(jnp.float32)
        means.append((csum[:, hi, gi] - csum[:, lo, gi]) / cnt[None, :, None])
    pooled = jnp.stack(means, axis=2) - hf
    mixed = jnp.einsum('bsgc,gcd->bsgd', pooled.astype(h.dtype), w_group).reshape(b, s, d)
    return mixed * scale


def conv_ffn(h, w_up, conv_w, conv_b, w_down):
    u = h @ w_up
    up = jnp.pad(u, ((0, 0), (1, 1), (0, 0)))
    c = up[:, :-2] * conv_w[0] + up[:, 1:-1] * conv_w[1] + up[:, 2:] * conv_w[2] + conv_b
    gate, val = jnp.split(c, 2, axis=-1)
    return (jax.nn.silu(gate) * val) @ w_down


def trunk(x, attn_norm, w_qkv, q_gain, k_gain, w_o, pool_norm, w_pool, pool_scale,
          ffn_norm, w_up, conv_w, conv_b, w_down):
    for i in range(DEPTH):
        j = i // N_MIXERS
        if i % N_MIXERS == 0:
            x = x + attention_mixer(rmsnorm(x, attn_norm[j]), w_qkv[j], q_gain[j], k_gain[j], w_o[j])
        else:
            x = x + pool_mixer(rmsnorm(x, pool_norm[j]), w_pool[j], pool_scale[j])
        x = x + conv_ffn(rmsnorm(x, ffn_norm[i]), w_up[i], conv_w[i], conv_b[i], w_down[i])
    return x


def setup_inputs(seed: int = 0) -> dict:
    key = jax.random.key(seed)
    ks = jax.random.split(key, 20)
    f32 = jnp.float32
    nrm = lambda k, shp, s: jax.random.normal(k, shp, f32) * s
    return {
        'x_prompt': nrm(ks[0], (BATCH, SEQ, D_MODEL), 1.0),
        'x_sample': nrm(ks[1], (DEC_BATCH, DEC_SEQ, D_MODEL), 1.0),
        'attn_norm': 1.0 + nrm(ks[2], (N_ATTN_LAYERS, D_MODEL), 0.05),
        'w_qkv': nrm(ks[3], (N_ATTN_LAYERS, D_MODEL, QKV_DIM), D_MODEL ** -0.5),
        'q_gain': 1.0 + nrm(ks[4], (N_ATTN_LAYERS, HEAD_DIM), 0.05),
        'k_gain': 1.0 + nrm(ks[5], (N_ATTN_LAYERS, HEAD_DIM), 0.05),
        'w_o': nrm(ks[6], (N_ATTN_LAYERS, N_HEADS * HEAD_DIM, D_MODEL), (N_HEADS * HEAD_DIM) ** -0.5),
        'pool_norm': 1.0 + nrm(ks[7], (N_POOL_LAYERS, D_MODEL), 0.05),
        'w_pool': nrm(ks[8], (N_POOL_LAYERS, N_POOL_GROUPS, POOL_GROUP_DIM, POOL_GROUP_DIM), POOL_GROUP_DIM ** -0.5),
        'pool_scale': 0.5 + nrm(ks[9], (N_POOL_LAYERS, D_MODEL), 0.1),
        'ffn_norm': 1.0 + nrm(ks[10], (DEPTH, D_MODEL), 0.05),
        'w_up': nrm(ks[11], (DEPTH, D_MODEL, 2 * D_FF), D_MODEL ** -0.5),
        'conv_w': nrm(ks[12], (DEPTH, CONV_W, 2 * D_FF), CONV_W ** -0.5),
        'conv_b': nrm(ks[13], (DEPTH, 2 * D_FF), 0.02),
        'w_down': nrm(ks[14], (DEPTH, D_FF, D_MODEL), D_FF ** -0.5),
    }


def reference(x_prompt, x_sample, attn_norm, w_qkv, q_gain, k_gain, w_o, pool_norm, w_pool,
              pool_scale, ffn_norm, w_up, conv_w, conv_b, w_down):
    y_prompt = trunk(x_prompt, attn_norm, w_qkv, q_gain, k_gain, w_o, pool_norm, w_pool,
                     pool_scale, ffn_norm, w_up, conv_w, conv_b, w_down)
    y_sample = trunk(x_sample, attn_norm, w_qkv, q_gain, k_gain, w_o, pool_norm, w_pool,
                     pool_scale, ffn_norm, w_up, conv_w, conv_b, w_down)
    return (y_prompt, y_sample)
```

```python
import functools
import math

import jax
import jax.numpy as jnp
from jax import lax
from jax.experimental import pallas as pl
from jax.experimental.pallas import tpu as pltpu

F32 = jnp.float32
BF16 = jnp.bfloat16

D_MODEL = 1024
N_HEADS = 8
N_KV_HEADS = 2
HEAD_DIM = 128
GROUP = N_HEADS // N_KV_HEADS
QKV_DIM = (N_HEADS + 2 * N_KV_HEADS) * HEAD_DIM
ROPE_THETA = 10000.0
ROPE_FREQS = HEAD_DIM // 4
GRID_W = 64
POOL_WINDOWS = (2, 4, 8, 16)
POOL_GROUP_DIM = D_MODEL // len(POOL_WINDOWS)
D_FF = 2816
EPS = 1e-6

SUBLANES = 8
HALO = SUBLANES
TOKEN_TILE = 512
Q_TILE = 512
KV_CHUNK = TOKEN_TILE
FF_CHUNK = 256
VMEM_LIMIT_BYTES = 56 * 1024 * 1024


def _params(n_axes):
    return pltpu.CompilerParams(dimension_semantics=("arbitrary",) * n_axes,
                                vmem_limit_bytes=VMEM_LIMIT_BYTES)


def _rms_rows(x, g):
    ms = jnp.mean(x * x, axis=-1, keepdims=True)
    return x * lax.rsqrt(ms + EPS) * g


def _rope_tables_t(seq_len):
    rows = seq_len // GRID_W
    row = jnp.repeat(jnp.arange(rows, dtype=F32), GRID_W)
    col = jnp.tile(jnp.arange(GRID_W, dtype=F32), rows)
    inv = ROPE_THETA ** (-jnp.arange(ROPE_FREQS, dtype=F32) / ROPE_FREQS)
    ang_r = inv[:, None] * row[None, :]
    ang_c = inv[:, None] * col[None, :]
    cos = jnp.concatenate([jnp.cos(ang_r)] * 2 + [jnp.cos(ang_c)] * 2, axis=0)
    sin = jnp.concatenate([-jnp.sin(ang_r), jnp.sin(ang_r), -jnp.sin(ang_c), jnp.sin(ang_c)], axis=0)
    return cos, sin


def _qkv_kernel(x_ref, g_ref, wt_ref, qg_ref, kg_ref, cos_ref, sin_ref, qt_ref, k_ref, vt_ref):
    h = _rms_rows(x_ref[0], g_ref[...]).astype(BF16)
    yt = lax.dot_general(wt_ref[...], h, (((1,), (1,)), ((), ())),
                         preferred_element_type=F32)
    tm = h.shape[0]
    cos = cos_ref[...]
    sin = sin_ref[...]
    qgain = jnp.broadcast_to(qg_ref[...], (HEAD_DIM, tm))
    kgain = jnp.broadcast_to(kg_ref[...], (HEAD_DIM, tm))
    q_scale = (HEAD_DIM ** -0.5) * math.log2(math.e)

    def norm_rope(blk, gain, mult):
        ms = jnp.mean(blk * blk, axis=0, keepdims=True)
        z = blk * (lax.rsqrt(ms + EPS) * mult) * gain
        q = HEAD_DIM // 4
        zs = jnp.concatenate([z[q:2 * q], z[0:q], z[3 * q:4 * q], z[2 * q:3 * q]], axis=0)
        return z * cos + zs * sin

    for hq in range(N_HEADS):
        blk = yt[hq * HEAD_DIM:(hq + 1) * HEAD_DIM]
        qt_ref[0, hq] = norm_rope(blk, qgain, q_scale).astype(BF16)
    k0 = N_HEADS * HEAD_DIM
    v0 = (N_HEADS + N_KV_HEADS) * HEAD_DIM
    for hk in range(N_KV_HEADS):
        blk = yt[k0 + hk * HEAD_DIM:k0 + (hk + 1) * HEAD_DIM]
        k_ref[0, hk] = norm_rope(blk, kgain, 1.0).T.astype(BF16)
        vt_ref[0, hk, 0] = yt[v0 + hk * HEAD_DIM:v0 + (hk + 1) * HEAD_DIM].astype(BF16)


def _qkv_proj(x, g, wt, qg, kg, cos_t, sin_t):
    b, s, d = x.shape
    tm = TOKEN_TILE
    nt = s // tm
    return pl.pallas_call(
        _qkv_kernel,
        grid=(b, nt),
        in_specs=[
            pl.BlockSpec((1, tm, d), lambda i, j: (i, j, 0)),
            pl.BlockSpec((1, d), lambda i, j: (0, 0)),
            pl.BlockSpec((QKV_DIM, d), lambda i, j: (0, 0)),
            pl.BlockSpec((HEAD_DIM, 1), lambda i, j: (0, 0)),
            pl.BlockSpec((HEAD_DIM, 1), lambda i, j: (0, 0)),
            pl.BlockSpec((HEAD_DIM, tm), lambda i, j: (0, j)),
            pl.BlockSpec((HEAD_DIM, tm), lambda i, j: (0, j)),
        ],
        out_specs=[
            pl.BlockSpec((1, N_HEADS, HEAD_DIM, tm), lambda i, j: (i, 0, 0, j)),
            pl.BlockSpec((1, N_KV_HEADS, tm, HEAD_DIM), lambda i, j: (i, 0, j, 0)),
            pl.BlockSpec((1, N_KV_HEADS, 1, HEAD_DIM, tm), lambda i, j: (i, 0, j, 0, 0)),
        ],
        out_shape=[
            jax.ShapeDtypeStruct((b, N_HEADS, HEAD_DIM, s), BF16),
            jax.ShapeDtypeStruct((b, N_KV_HEADS, s, HEAD_DIM), BF16),
            jax.ShapeDtypeStruct((b, N_KV_HEADS, nt, HEAD_DIM, tm), BF16),
        ],
        compiler_params=_params(2),
        name="qkv_proj",
    )(x, g, wt, qg, kg, cos_t, sin_t)


def _attn_kernel(qt_ref, k_ref, vt_ref, o_ref, m_sc, l_sc, acc_sc, *, n_chunks):
    qt = qt_ref[0, 0]
    m_sc[...] = jnp.full_like(m_sc, -jnp.inf)
    l_sc[...] = jnp.zeros_like(l_sc)
    acc_sc[...] = jnp.zeros_like(acc_sc)

    def body(c, carry):
        off = pl.multiple_of(c * KV_CHUNK, KV_CHUNK)
        kc = k_ref[0, 0, pl.ds(off, KV_CHUNK), :]
        vc = vt_ref[0, 0, c]
        st = jnp.dot(kc, qt, preferred_element_type=F32)
        m_old = m_sc[...]
        m_new = jnp.maximum(m_old, jnp.max(st, axis=0, keepdims=True))
        alpha = jnp.exp2(m_old - m_new)
        p = jnp.exp2(st - m_new)
        l_sc[...] = alpha * l_sc[...] + jnp.sum(p, axis=0, keepdims=True)
        acc_sc[...] = alpha * acc_sc[...] + jnp.dot(vc, p.astype(BF16), preferred_element_type=F32)
        m_sc[...] = m_new
        return carry

    lax.fori_loop(0, n_chunks, body, 0)
    o_ref[0] = (acc_sc[...] / l_sc[...]).T.astype(BF16)


def _attention(qt, k, vt):
    b, _, _, s = qt.shape
    tq = Q_TILE
    nq = s // tq
    n_chunks = s // KV_CHUNK
    return pl.pallas_call(
        functools.partial(_attn_kernel, n_chunks=n_chunks),
        grid=(b, N_HEADS, nq),
        in_specs=[
            pl.BlockSpec((1, 1, HEAD_DIM, tq), lambda i, h, j: (i, h, 0, j)),
            pl.BlockSpec((1, 1, s, HEAD_DIM), lambda i, h, j: (i, h // GROUP, 0, 0)),
            pl.BlockSpec((1, 1, n_chunks, HEAD_DIM, KV_CHUNK), lambda i, h, j: (i, h // GROUP, 0, 0, 0)),
        ],
        out_specs=pl.BlockSpec((1, tq, HEAD_DIM), lambda i, h, j: (i, j, h)),
        out_shape=jax.ShapeDtypeStruct((b, s, N_HEADS * HEAD_DIM), BF16),
        scratch_shapes=[
            pltpu.VMEM((1, tq), F32),
            pltpu.VMEM((1, tq), F32),
            pltpu.VMEM((HEAD_DIM, tq), F32),
        ],
        compiler_params=_params(3),
        name="flash_attn",
    )(qt, k, vt)


def _oproj_kernel(x_ref, o_ref, w_ref, y_ref):
    y_ref[0] = x_ref[0] + jnp.dot(o_ref[0], w_ref[...], preferred_element_type=F32)


def _out_proj(x, o, w):
    b, s, d = x.shape
    tm = TOKEN_TILE
    return pl.pallas_call(
        _oproj_kernel,
        grid=(b, s // tm),
        in_specs=[
            pl.BlockSpec((1, tm, d), lambda i, j: (i, j, 0)),
            pl.BlockSpec((1, tm, d), lambda i, j: (i, j, 0)),
            pl.BlockSpec((d, d), lambda i, j: (0, 0)),
        ],
        out_specs=pl.BlockSpec((1, tm, d), lambda i, j: (i, j, 0)),
        out_shape=jax.ShapeDtypeStruct((b, s, d), F32),
        compiler_params=_params(2),
        name="out_proj",
    )(x, o, w)


def _halo_specs(tm, d):
    r = tm // HALO

    def prev_map(i, j):
        return (i, jnp.maximum(j * r - 1, 0), 0)

    def next_map(n_blocks):
        return lambda i, j: (i, jnp.minimum((j + 1) * r, n_blocks - 1), 0)

    return prev_map, next_map


def _normed_with_halo(xp_ref, x_ref, xn_ref, g):
    j = pl.program_id(1)
    last = pl.num_programs(1) - 1
    hp = jnp.where(j > 0, _rms_rows(xp_ref[0], g), 0.0)
    hn = jnp.where(j < last, _rms_rows(xn_ref[0], g), 0.0)
    return jnp.concatenate([hp, _rms_rows(x_ref[0], g), hn], axis=0)


def _shift_rows(u, k):
    return pltpu.roll(u, k % u.shape[0], 0)


def _ffn_kernel(xp_ref, x_ref, xn_ref, g_ref, wug_ref, wuv_ref, cwg_ref, cwv_ref, cbg_ref, cbv_ref,
                wd_ref, y_ref, act_sc):
    tm = x_ref.shape[1]
    h = _normed_with_halo(xp_ref, x_ref, xn_ref, g_ref[...]).astype(BF16)

    def conv(u, w, bias):
        c = _shift_rows(u, 1) * w[0:1] + u * w[1:2] + _shift_rows(u, -1) * w[2:3] + bias
        return c[HALO:HALO + tm]

    n_chunks = wug_ref.shape[0]
    for c in range(n_chunks):
        ug = jnp.dot(h, wug_ref[c], preferred_element_type=F32)
        uv = jnp.dot(h, wuv_ref[c], preferred_element_type=F32)
        gate = conv(ug, cwg_ref[c], cbg_ref[c])
        val = conv(uv, cwv_ref[c], cbv_ref[c])
        act = gate * jax.nn.sigmoid(gate) * val
        act_sc[:, c * FF_CHUNK:(c + 1) * FF_CHUNK] = act.astype(BF16)
    y_ref[0] = x_ref[0] + jnp.dot(act_sc[...], wd_ref[...], preferred_element_type=F32)


def _conv_ffn(x, g, wug, wuv, cwg, cwv, cbg, cbv, wd):
    b, s, d = x.shape
    tm = TOKEN_TILE
    prev_map, next_map = _halo_specs(tm, d)
    const = dict(pipeline_mode=pl.Buffered(1))
    nch = wug.shape[0]
    return pl.pallas_call(
        _ffn_kernel,
        grid=(b, s // tm),
        in_specs=[
            pl.BlockSpec((1, HALO, d), prev_map),
            pl.BlockSpec((1, tm, d), lambda i, j: (i, j, 0)),
            pl.BlockSpec((1, HALO, d), next_map(s // HALO)),
            pl.BlockSpec((1, d), lambda i, j: (0, 0)),
            pl.BlockSpec((nch, d, FF_CHUNK), lambda i, j: (0, 0, 0), **const),
            pl.BlockSpec((nch, d, FF_CHUNK), lambda i, j: (0, 0, 0), **const),
            pl.BlockSpec((nch, 3, FF_CHUNK), lambda i, j: (0, 0, 0)),
            pl.BlockSpec((nch, 3, FF_CHUNK), lambda i, j: (0, 0, 0)),
            pl.BlockSpec((nch, 1, FF_CHUNK), lambda i, j: (0, 0, 0)),
            pl.BlockSpec((nch, 1, FF_CHUNK), lambda i, j: (0, 0, 0)),
            pl.BlockSpec((D_FF, d), lambda i, j: (0, 0), **const),
        ],
        out_specs=pl.BlockSpec((1, tm, d), lambda i, j: (i, j, 0)),
        out_shape=jax.ShapeDtypeStruct((b, s, d), F32),
        scratch_shapes=[pltpu.VMEM((tm, D_FF), BF16)],
        compiler_params=_params(2),
        name="conv_ffn",
    )(x, x, x, g, wug, wuv, cwg, cwv, cbg, cbv, wd)


def _pool_kernel(xp_ref, x_ref, xn_ref, g_ref, wg_ref, sc_ref, y_ref, *, seq_len):
    tm = x_ref.shape[1]
    h = _normed_with_halo(xp_ref, x_ref, xn_ref, g_ref[...])
    t = pl.program_id(1) * tm + lax.broadcasted_iota(jnp.int32, (tm, 1), 0)
    outs = []
    for gi, w in enumerate(POOL_WINDOWS):
        hg = h[:, gi * POOL_GROUP_DIM:(gi + 1) * POOL_GROUP_DIM]
        s = hg + _shift_rows(hg, 1)
        half = 1
        while 2 * half < w:
            s = _shift_rows(s, half) + _shift_rows(s, -half)
            half *= 2
        cnt = (jnp.minimum(t + w // 2, seq_len) - jnp.maximum(t - w // 2, 0)).astype(F32)
        pooled = s[HALO:HALO + tm] / cnt - hg[HALO:HALO + tm]
        outs.append(jnp.dot(pooled.astype(BF16), wg_ref[gi], preferred_element_type=F32))
    y_ref[0] = x_ref[0] + jnp.concatenate(outs, axis=1) * sc_ref[...]


def _pool_mixer(x, g, wg, scale):
    b, s, d = x.shape
    tm = TOKEN_TILE
    prev_map, next_map = _halo_specs(tm, d)
    ng = len(POOL_WINDOWS)
    return pl.pallas_call(
        functools.partial(_pool_kernel, seq_len=s),
        grid=(b, s // tm),
        in_specs=[
            pl.BlockSpec((1, HALO, d), prev_map),
            pl.BlockSpec((1, tm, d), lambda i, j: (i, j, 0)),
            pl.BlockSpec((1, HALO, d), next_map(s // HALO)),
            pl.BlockSpec((1, d), lambda i, j: (0, 0)),
            pl.BlockSpec((ng, POOL_GROUP_DIM, POOL_GROUP_DIM), lambda i, j: (0, 0, 0)),
            pl.BlockSpec((1, d), lambda i, j: (0, 0)),
        ],
        out_specs=pl.BlockSpec((1, tm, d), lambda i, j: (i, j, 0)),
        out_shape=jax.ShapeDtypeStruct((b, s, d), F32),
        compiler_params=_params(2),
        name="pool_mixer",
    )(x, x, x, g, wg, scale)


def _ffn_weights(w_up, conv_w, conv_b, w_down):
    nch = D_FF // FF_CHUNK

    def cols(a):
        return a.reshape(a.shape[0], nch, FF_CHUNK).transpose(1, 0, 2)

    return (cols(w_up[:, :D_FF]).astype(BF16), cols(w_up[:, D_FF:]).astype(BF16),
            cols(conv_w[:, :D_FF]), cols(conv_w[:, D_FF:]),
            cols(conv_b[None, :D_FF]), cols(conv_b[None, D_FF:]),
            w_down.astype(BF16))


def _trunk(x, attn, pool, ffn0, ffn1):
    attn_g, wt, qg, kg, wo = attn
    cos_t, sin_t = _rope_tables_t(x.shape[1])
    qt, k, vt = _qkv_proj(x, attn_g, wt, qg, kg, cos_t, sin_t)
    o = _attention(qt, k, vt)
    x = _out_proj(x, o, wo)
    x = _conv_ffn(x, *ffn0)
    x = _pool_mixer(x, *pool)
    x = _conv_ffn(x, *ffn1)
    return x


def kernel(x_prompt, x_sample, attn_norm, w_qkv, q_gain, k_gain, w_o, pool_norm, w_pool, pool_scale,
           ffn_norm, w_up, conv_w, conv_b, w_down):
    attn = (attn_norm[0][None, :], w_qkv[0].T.astype(BF16), q_gain[0][:, None], k_gain[0][:, None],
            w_o[0].astype(BF16))
    pool = (pool_norm[0][None, :], w_pool[0].astype(BF16), pool_scale[0][None, :])
    ffn = [(ffn_norm[i][None, :],) + _ffn_weights(w_up[i], conv_w[i], conv_b[i], w_down[i])
           for i in range(2)]
    return (_trunk(x_prompt, attn, pool, ffn[0], ffn[1]),
            _trunk(x_sample, attn, pool, ffn[0], ffn[1]))
```
